```python
import math
import jax, jax.numpy as jnp
from jax import lax
import numpy as np

D_MODEL = 1024
BATCH = 2
SEQ = 8192
DEPTH = 4

CHUNK = 64
HEAD_DIM = 64
SSM_WIDTH = D_MODEL // 4
ATT_WIDTH = (D_MODEL - SSM_WIDTH) // 2
RET_WIDTH = D_MODEL - SSM_WIDTH - ATT_WIDTH
ATT_HEADS = ATT_WIDTH // HEAD_DIM
RET_HEADS = RET_WIDTH // HEAD_DIM
IN_COLS = 3 * ATT_WIDTH + 4 * RET_WIDTH + SSM_WIDTH
ATT_LEFT_CHUNKS = 8
ATT_BAND = (ATT_LEFT_CHUNKS + 1) * CHUNK
REL_CLIP = 256
ROPE_BASE = 10000.0
SSM_GROUP = 16
SSM_GROUPS = SSM_WIDTH // SSM_GROUP
SSM_STATE = 64
PEER_HEADS = 8
PEER_TOPK = 16
PEER_NKEYS = 128
PEER_QDIM = 256
PEER_N = PEER_NKEYS * PEER_NKEYS
PEER_BLOCK = 128
EPS = 1e-6

kernel_name = 'hybrid_streaming_encoder_block'


def _rms(x):
    xf = x.astype(jnp.float32)
    return xf * lax.rsqrt(jnp.mean(xf * xf, axis=-1, keepdims=True) + EPS)


def rms_norm(x, gain):
    return (_rms(x) * gain.astype(jnp.float32)).astype(x.dtype)


def chunk_attention(q, k, v, rel_bias):
    b, l, h, dh = q.shape
    nc = l // CHUNK
    qc = q.reshape(b, nc, CHUNK, h, dh)
    kc = k.reshape(b, nc, CHUNK, h, dh)
    vc = v.reshape(b, nc, CHUNK, h, dh)
    pad = ((0, 0), (ATT_LEFT_CHUNKS, 0), (0, 0), (0, 0), (0, 0))
    kp = jnp.pad(kc, pad)
    vp = jnp.pad(vc, pad)
    kb = jnp.concatenate([kp[:, j:j + nc] for j in range(ATT_LEFT_CHUNKS + 1)], axis=2)
    vb = jnp.concatenate([vp[:, j:j + nc] for j in range(ATT_LEFT_CHUNKS + 1)], axis=2)
    scores = jnp.einsum('bnqhd,bnkhd->bnhqk', qc, kb).astype(jnp.float32) * (dh ** -0.5)
    qpos = jnp.arange(CHUNK) + ATT_LEFT_CHUNKS * CHUNK
    kpos = jnp.arange(ATT_BAND)
    rel = jnp.clip(qpos[:, None] - kpos[None, :], -REL_CLIP, REL_CLIP) + REL_CLIP
    bias = rel_bias.astype(jnp.float32)[:, rel]
    key_chunk = jnp.arange(nc)[:, None] + (kpos // CHUNK)[None, :] - ATT_LEFT_CHUNKS
    valid = (key_chunk >= 0)[None, :, None, None, :]
    scores = jnp.where(valid, scores + bias[None, None], -1e30)
    p = jax.nn.softmax(scores, axis=-1).astype(v.dtype)
    out = jnp.einsum('bnhqk,bnkhd->bnqhd', p, vb)
    return out.reshape(b, l, h * dh)


def rotary(x, pos):
    d = x.shape[-1]
    half = d // 2
    inv = ROPE_BASE ** (-jnp.arange(half, dtype=jnp.float32) / half)
    ang = pos.astype(jnp.float32)[:, None] * inv[None, :]
    cos = jnp.cos(ang)[None, :, None, :]
    sin = jnp.sin(ang)[None, :, None, :]
    xf = x.astype(jnp.float32)
    x1, x2 = xf[..., :half], xf[..., half:]
    return jnp.concatenate([x1 * cos - x2 * sin, x1 * sin + x2 * cos], axis=-1)


def retention(q, k, v, g):
    b, l, h, d = q.shape
    nc = l // CHUNK
    pos = jnp.arange(l)
    qf = rotary(q, pos)
    kf = rotary(k, pos) * (d ** -0.5)
    vf = v.astype(jnp.float32)
    log_g = jnp.log1p(-jnp.power(2.0, -5.0 - jnp.arange(h, dtype=jnp.float32)))
    idx = jnp.arange(CHUNK, dtype=jnp.float32)
    intra_decay = jnp.exp(log_g[:, None, None] * jnp.abs(idx[:, None] - idx[None, :]))
    k_decay = jnp.exp(log_g[:, None] * (CHUNK - 1 - idx)[None, :])
    q_decay = jnp.exp(log_g[:, None] * (idx + 1.0)[None, :])
    chunk_decay = jnp.exp(log_g * CHUNK)[None, :, None, None]
    qc = qf.reshape(b, nc, CHUNK, h, d)
    kc = kf.reshape(b, nc, CHUNK, h, d)
    vc = vf.reshape(b, nc, CHUNK, h, d)
    s = jnp.einsum('bnqhd,bnkhd->bnhqk', qc, kc) * intra_decay
    y_intra = jnp.einsum('bnhqk,bnkhe->bnqhe', s, vc)
    upd = jnp.einsum('bnkhd,bnkhe,hk->nbhde', kc, vc, k_decay)

    def step(state, u):
        return chunk_decay * state + u, state

    _, s_prev = lax.scan(step, jnp.zeros((b, h, d, d), jnp.float32), upd)
    y_cross = jnp.einsum('bnqhd,nbhde,hq->bnqhe', qc, s_prev, q_decay)
    y = y_intra + y_cross
    mu = jnp.mean(y, axis=-1, keepdims=True)
    var = jnp.mean(jnp.square(y - mu), axis=-1, keepdims=True)
    y = (y - mu) * lax.rsqrt(var + EPS)
    y = y.reshape(b, l, h * d) * jax.nn.silu(g.astype(jnp.float32))
    return y.astype(v.dtype)


def s5_ssm(u, lam_re, lam_im, log_dt, b_re, b_im, c_re, c_im, d_skip, w_glu):
    bsz, l, w = u.shape
    uf = u.astype(jnp.float32).reshape(bsz, l, SSM_GROUPS, SSM_GROUP)
    dt = jnp.exp(log_dt.astype(jnp.float32))[:, None]
    lr = lam_re.astype(jnp.float32)
    li = lam_im.astype(jnp.float32)
    mag = jnp.exp(lr * dt)
    ab_re = mag * jnp.cos(li * dt)
    ab_im = mag * jnp.sin(li * dt)
    den = lr * lr + li * li
    nr, ni = ab_re - 1.0, ab_im
    f_re = (nr * lr + ni * li) / den
    f_im = (ni * lr - nr * li) / den
    br, bi = b_re.astype(jnp.float32), b_im.astype(jnp.float32)
    bb_re = f_re[..., None] * br - f_im[..., None] * bi
    bb_im = f_re[..., None] * bi + f_im[..., None] * br
    bu_re = jnp.einsum('gpc,blgc->blgp', bb_re, uf)
    bu_im = jnp.einsum('gpc,blgc->blgp', bb_im, uf)
    a_re = jnp.broadcast_to(ab_re, bu_re.shape)
    a_im = jnp.broadcast_to(ab_im, bu_im.shape)

    def combine(e1, e2):
        ar1, ai1, br1, bi1 = e1
        ar2, ai2, br2, bi2 = e2
        return (ar2 * ar1 - ai2 * ai1,
                ar2 * ai1 + ai2 * ar1,
                ar2 * br1 - ai2 * bi1 + br2,
                ar2 * bi1 + ai2 * br1 + bi2)

    _, _, x_re, x_im = lax.associative_scan(combine, (a_re, a_im, bu_re, bu_im), axis=1)
    y = (jnp.einsum('gcp,blgp->blgc', c_re.astype(jnp.float32), x_re)
         - jnp.einsum('gcp,blgp->blgc', c_im.astype(jnp.float32), x_im)
         + d_skip.astype(jnp.float32).reshape(SSM_GROUPS, SSM_GROUP) * uf)
    y = jax.nn.gelu(y.reshape(bsz, l, w))
    z = y @ w_glu.astype(jnp.float32)
    out = z[..., :w] * jax.nn.sigmoid(z[..., w:])
    return out.astype(u.dtype)


def peer(h, w_q, subkeys, u_tab, v_tab):
    b, l, dm = h.shape
    t = b * l
    hf = h.reshape(t, dm)
    q = (hf @ w_q).astype(jnp.float32).reshape(t, PEER_HEADS, 2, PEER_QDIM // 2)
    s = jnp.einsum('thsd,hskd->thsk', q, subkeys.astype(jnp.float32))
    top_s, top_i = lax.top_k(s, PEER_TOPK)
    cand_s = top_s[:, :, 0, :, None] + top_s[:, :, 1, None, :]
    cand_i = top_i[:, :, 0, :, None] * PEER_NKEYS + top_i[:, :, 1, None, :]
    best_s, best_pos = lax.top_k(cand_s.reshape(t, PEER_HEADS, PEER_TOPK * PEER_TOPK), PEER_TOPK)
    expert = jnp.take_along_axis(cand_i.reshape(t, PEER_HEADS, PEER_TOPK * PEER_TOPK), best_pos, axis=-1)
    gate = jax.nn.softmax(best_s, axis=-1)
    nblk = t // PEER_BLOCK

    def block(args):
        hb, eb, gb = args
        ub = jnp.take(u_tab, eb, axis=0)
        vb = jnp.take(v_tab, eb, axis=0)
        act = jax.nn.gelu(jnp.einsum('thkd,td->thk', ub, hb).astype(jnp.float32))
        return jnp.einsum('thk,thkd->td', (gb * act).astype(vb.dtype), vb)

    out = lax.map(block, (hf.reshape(nblk, PEER_BLOCK, dm),
                          expert.reshape(nblk, PEER_BLOCK, PEER_HEADS, PEER_TOPK),
                          gate.reshape(nblk, PEER_BLOCK, PEER_HEADS, PEER_TOPK)))
    return out.reshape(b, l, dm).astype(h.dtype)


def setup_inputs(seed: int = 0) -> dict:
    key = jax.random.key(seed)
    ks = jax.random.split(key, 26)
    f32 = jnp.float32
    nrm = lambda k, shape, scale: jax.random.normal(k, shape, f32) * scale
    lam_re = -0.5 + nrm(ks[7], (DEPTH, SSM_GROUPS, SSM_STATE), 0.01)
    lam_im = (jnp.pi * jnp.arange(SSM_STATE, dtype=f32))[None, None, :] + nrm(ks[8], (DEPTH, SSM_GROUPS, SSM_STATE), 0.01)
    log_dt = jax.random.uniform(ks[9], (DEPTH, SSM_GROUPS), f32, math.log(1e-3), math.log(1e-1))
    return {
        'x': nrm(ks[0], (BATCH, SEQ, D_MODEL), 1.0),
        'c': nrm(ks[1], (BATCH, D_MODEL), 1.0),
        'w_ada': nrm(ks[2], (DEPTH, D_MODEL, 6 * D_MODEL), 0.5 * D_MODEL ** -0.5),
        'b_ada': nrm(ks[3], (DEPTH, 6 * D_MODEL), 0.02),
        'norm_mix': 1.0 + nrm(ks[4], (DEPTH, D_MODEL), 0.05),
        'norm_ffn': 1.0 + nrm(ks[5], (DEPTH, D_MODEL), 0.05),
        'w_in': nrm(ks[6], (DEPTH, D_MODEL, IN_COLS), D_MODEL ** -0.5),
        'rel_bias': nrm(ks[10], (DEPTH, ATT_HEADS, 2 * REL_CLIP + 1), 0.2),
        'ssm_lambda_re': lam_re,
        'ssm_lambda_im': lam_im,
        'ssm_log_dt': log_dt,
        'ssm_b_re': nrm(ks[11], (DEPTH, SSM_GROUPS, SSM_STATE, SSM_GROUP), (2 * SSM_GROUP) ** -0.5),
        'ssm_b_im': nrm(ks[12], (DEPTH, SSM_GROUPS, SSM_STATE, SSM_GROUP), (2 * SSM_GROUP) ** -0.5),
        'ssm_c_re': nrm(ks[13], (DEPTH, SSM_GROUPS, SSM_GROUP, SSM_STATE), (2 * SSM_STATE) ** -0.5),
        'ssm_c_im': nrm(ks[14], (DEPTH, SSM_GROUPS, SSM_GROUP, SSM_STATE), (2 * SSM_STATE) ** -0.5),
        'ssm_d': nrm(ks[15], (DEPTH, SSM_WIDTH), 1.0),
        'w_glu': nrm(ks[16], (DEPTH, SSM_WIDTH, 2 * SSM_WIDTH), SSM_WIDTH ** -0.5),
        'merge_gain': 1.0 + nrm(ks[17], (DEPTH, D_MODEL), 0.05),
        'w_out': nrm(ks[18], (DEPTH, D_MODEL, D_MODEL), D_MODEL ** -0.5),
        'peer_w_q': nrm(ks[19], (DEPTH, D_MODEL, PEER_HEADS * PEER_QDIM), D_MODEL ** -0.5),
        'peer_subkeys': nrm(ks[20], (DEPTH, PEER_HEADS, 2, PEER_NKEYS, PEER_QDIM // 2), (PEER_QDIM // 2) ** -0.5),
        'peer_u': nrm(ks[21], (DEPTH, PEER_N, D_MODEL), D_MODEL ** -0.5),
        'peer_v': nrm(ks[22], (DEPTH, PEER_N, D_MODEL), 0.5),
        'final_norm': 1.0 + nrm(ks[23], (D_MODEL,), 0.05),
    }


def reference(x, c, w_ada, b_ada, norm_mix, norm_ffn, w_in, rel_bias, ssm_lambda_re, ssm_lambda_im,
              ssm_log_dt, ssm_b_re, ssm_b_im, ssm_c_re, ssm_c_im, ssm_d, w_glu, merge_gain, w_out,
              peer_w_q, peer_subkeys, peer_u, peer_v, final_norm):
    b, l, dm = x.shape
    sizes = [ATT_WIDTH] * 3 + [RET_WIDTH] * 4 + [SSM_WIDTH]
    splits = [int(v) for v in np.cumsum(sizes)[:-1]]
    cond = jax.nn.silu(c)
    for layer in range(DEPTH):
        mod = (cond @ w_ada[layer] + b_ada[layer])[:, None, :]
        sh1, sc1, g1, sh2, sc2, g2 = jnp.split(mod, 6, axis=-1)
        hmix = rms_norm(x, norm_mix[layer]) * (1.0 + sc1) + sh1
        proj = hmix @ w_in[layer]
        qa, ka, va, qr, kr, vr, gr, us = jnp.split(proj, splits, axis=-1)
        hs = lambda t: t.reshape(b, l, -1, HEAD_DIM)
        att = chunk_attention(hs(qa), hs(ka), hs(va), rel_bias[layer])
        ret = retention(hs(qr), hs(kr), hs(vr), gr)
        ssm = s5_ssm(us, ssm_lambda_re[layer], ssm_lambda_im[layer], ssm_log_dt[layer],
                     ssm_b_re[layer], ssm_b_im[layer], ssm_c_re[layer], ssm_c_im[layer],
                     ssm_d[layer], w_glu[layer])
        merged = jnp.concatenate([_rms(att), ret.astype(jnp.float32), _rms(ssm)], axis=-1)
        merged = (merged * merge_gain[layer].astype(jnp.float32)).astype(x.dtype)
        x = x + g1 * (merged @ w_out[layer])
        hffn = rms_norm(x, norm_ffn[layer]) * (1.0 + sc2) + sh2
        x = x + g2 * peer(hffn, peer_w_q[layer], peer_subkeys[layer], peer_u[layer], peer_v[layer])
    return rms_norm(x, final_norm)
```

```python
import functools
import math

import numpy as np
import jax
import jax.numpy as jnp
from jax import lax
from jax.experimental import pallas as pl
from jax.experimental.pallas import tpu as pltpu

F32 = jnp.float32
BF16 = jnp.bfloat16

D_MODEL = 1024
DEPTH = 4
CHUNK = 64
HEAD_DIM = 64
SSM_WIDTH = D_MODEL // 4
ATT_WIDTH = (D_MODEL - SSM_WIDTH) // 2
RET_WIDTH = D_MODEL - SSM_WIDTH - ATT_WIDTH
ATT_HEADS = ATT_WIDTH // HEAD_DIM
RET_HEADS = RET_WIDTH // HEAD_DIM
ATT_LEFT_CHUNKS = 8
REL_CLIP = 256
ROPE_BASE = 10000.0
SSM_GROUP = 16
SSM_GROUPS = SSM_WIDTH // SSM_GROUP
SSM_STATE = 64
PEER_HEADS = 8
PEER_TOPK = 16
PEER_NKEYS = 128
PEER_QDIM = 256
PEER_N = PEER_NKEYS * PEER_NKEYS
EPS = 1e-6
NEG = -1e30

LANES = 128
VMEM_LIMIT = 56 * 1024 * 1024

TM_PROJ = 512
ATT_QB = 256
ATT_KB = 3 * ATT_QB
RET_RB = 512
SSM_LC = 8
SSM_FLAT = SSM_LC * SSM_WIDTH
SSM_NSTATE = SSM_GROUPS * SSM_STATE
TM_ROUTER = 256
TM_PEER = 512
TE_PEER = 1024

_NT = (((1,), (1,)), ((), ()))


def _cparams(sem):
    return pltpu.CompilerParams(dimension_semantics=sem, vmem_limit_bytes=VMEM_LIMIT)


def _gelu(x):
    return 0.5 * x * (1.0 + jnp.tanh(0.7978845608028654 * (x + 0.044715 * (x * x * x))))


def _rms_rows(x):
    return x * lax.rsqrt(jnp.mean(x * x, axis=-1, keepdims=True) + EPS)


def _ada_kernel(c_ref, w_ref, b_ref, o_ref):
    c = c_ref[...]
    cond = c * jax.nn.sigmoid(c)
    o_ref[0] = jnp.dot(cond.astype(BF16), w_ref[0].astype(BF16), preferred_element_type=F32) + b_ref[0]


def _ada_mod(c, w_ada, b_ada):
    b = c.shape[0]
    rows = 8
    c_pad = jnp.zeros((rows, D_MODEL), F32).at[:b].set(c)
    out = pl.pallas_call(
        _ada_kernel,
        out_shape=jax.ShapeDtypeStruct((DEPTH, rows, 6 * D_MODEL), F32),
        grid=(DEPTH, 6),
        in_specs=[
            pl.BlockSpec((rows, D_MODEL), lambda l, j: (0, 0)),
            pl.BlockSpec((1, D_MODEL, D_MODEL), lambda l, j: (l, 0, j)),
            pl.BlockSpec((1, 1, D_MODEL), lambda l, j: (l, 0, j)),
        ],
        out_specs=pl.BlockSpec((1, rows, D_MODEL), lambda l, j: (l, 0, j)),
        compiler_params=_cparams(("arbitrary", "arbitrary")),
        name="ada_mod",
    )(c_pad, w_ada, b_ada.reshape(DEPTH, 1, 6 * D_MODEL))
    return out[:, :b].reshape(DEPTH, b, 6, D_MODEL)


def _inproj_kernel(x_ref, mod_ref, gain_ref, wa_ref, wr_ref, wu_ref, oa_ref, or_ref, ou_ref):
    h = _rms_rows(x_ref[...]) * gain_ref[...]
    h = h * (1.0 + mod_ref[0, 1:2, :]) + mod_ref[0, 0:1, :]
    hb = h.astype(BF16)
    oa_ref[...] = jnp.dot(hb, wa_ref[...], preferred_element_type=F32).astype(BF16)
    or_ref[...] = jnp.dot(hb, wr_ref[...], preferred_element_type=F32).astype(BF16)
    ou_ref[...] = jnp.dot(hb, wu_ref[...], preferred_element_type=F32).astype(BF16)


def _inproj(x2, mod_l, gain, w_att, w_ret, w_u, seq):
    t = x2.shape[0]
    tm = TM_PROJ
    per_b = seq // tm
    na, nr, nu = w_att.shape[1], w_ret.shape[1], w_u.shape[1]
    return pl.pallas_call(
        _inproj_kernel,
        out_shape=(jax.ShapeDtypeStruct((t, na), BF16), jax.ShapeDtypeStruct((t, nr), BF16),
                   jax.ShapeDtypeStruct((t, nu), BF16)),
        grid=(t // tm,),
        in_specs=[
            pl.BlockSpec((tm, D_MODEL), lambda i: (i, 0)),
            pl.BlockSpec((1, 6, D_MODEL), lambda i: (i // per_b, 0, 0)),
            pl.BlockSpec((1, D_MODEL), lambda i: (0, 0)),
            pl.BlockSpec((D_MODEL, na), lambda i: (0, 0)),
            pl.BlockSpec((D_MODEL, nr), lambda i: (0, 0)),
            pl.BlockSpec((D_MODEL, nu), lambda i: (0, 0)),
        ],
        out_specs=(pl.BlockSpec((tm, na), lambda i: (i, 0)), pl.BlockSpec((tm, nr), lambda i: (i, 0)),
                   pl.BlockSpec((tm, nu), lambda i: (i, 0))),
        compiler_params=_cparams(("arbitrary",)),
        name="inproj",
    )(x2, mod_l, gain, w_att, w_ret, w_u)


def _att_kernel(q_ref, k0_ref, k1_ref, k2_ref, v0_ref, v1_ref, v2_ref, bias_ref, o_ref):
    i = pl.program_id(1)
    n_invalid = jnp.maximum(2 - i, 0) * ATT_QB
    col_ok = lax.broadcasted_iota(jnp.int32, (ATT_QB, ATT_KB), 1) >= n_invalid
    low = lax.broadcasted_iota(jnp.int32, (ATT_QB, LANES), 1) < HEAD_DIM
    for p in range(ATT_HEADS // 2):
        sl = slice(p * LANES, (p + 1) * LANES)
        q = q_ref[:, sl]
        k = jnp.concatenate([k0_ref[:, sl], k1_ref[:, sl], k2_ref[:, sl]], axis=0)
        v = jnp.concatenate([v0_ref[:, sl], v1_ref[:, sl], v2_ref[:, sl]], axis=0)
        outs = []
        for par in range(2):
            qm = jnp.where(low if par == 0 else jnp.logical_not(low), q, jnp.zeros_like(q))
            s = lax.dot_general(qm, k, _NT, preferred_element_type=F32)
            s = s * (HEAD_DIM ** -0.5) + bias_ref[2 * p + par]
            s = jnp.where(col_ok, s, NEG)
            m = jnp.max(s, axis=-1, keepdims=True)
            e = jnp.exp(s - m)
            denom = jnp.sum(e, axis=-1, keepdims=True)
            o = jnp.dot(e.astype(BF16), v, preferred_element_type=F32)
            outs.append(o / denom)
        o_ref[:, sl] = jnp.where(low, outs[0], outs[1])


def _att_bias(rel_bias_l):
    q = np.arange(ATT_QB)[:, None]
    k = np.arange(ATT_KB)[None, :]
    idx = np.clip(ATT_LEFT_CHUNKS * CHUNK + q - k, -REL_CLIP, REL_CLIP) + REL_CLIP
    qc, kc = q // CHUNK, k // CHUNK
    in_band = (kc >= qc) & (kc <= qc + ATT_LEFT_CHUNKS)
    return jnp.where(jnp.asarray(in_band)[None], rel_bias_l[:, idx], NEG).astype(F32)


def _attention(qkv, bias, batch, seq):
    t = qkv.shape[0]
    nq = seq // ATT_QB
    w = ATT_WIDTH

    def qmap(b, i):
        return (b * nq + i, 0)

    def kvmap(back, colblk):
        return lambda b, i: (b * nq + jnp.maximum(i - back, 0), colblk)

    return pl.pallas_call(
        _att_kernel,
        out_shape=jax.ShapeDtypeStruct((t, w), F32),
        grid=(batch, nq),
        in_specs=[
            pl.BlockSpec((ATT_QB, w), qmap),
            pl.BlockSpec((ATT_QB, w), kvmap(2, 1)),
            pl.BlockSpec((ATT_QB, w), kvmap(1, 1)),
            pl.BlockSpec((ATT_QB, w), kvmap(0, 1)),
            pl.BlockSpec((ATT_QB, w), kvmap(2, 2)),
            pl.BlockSpec((ATT_QB, w), kvmap(1, 2)),
            pl.BlockSpec((ATT_QB, w), kvmap(0, 2)),
            pl.BlockSpec((ATT_HEADS, ATT_QB, ATT_KB), lambda b, i: (0, 0, 0)),
        ],
        out_specs=pl.BlockSpec((ATT_QB, w), qmap),
        compiler_params=_cparams(("arbitrary", "arbitrary")),
        name="chunk_attention",
    )(qkv, qkv, qkv, qkv, qkv, qkv, qkv, bias)


def _ret_kernel(q_ref, k_ref, v_ref, g_ref, cos_ref, sin_ref, intra_ref, kdec_ref, qdec_ref, cmat_ref,
                o_ref, state_ref):
    @pl.when(pl.program_id(1) == 0)
    def _():
        state_ref[...] = jnp.zeros_like(state_ref)

    lane = lax.broadcasted_iota(jnp.int32, (CHUNK, LANES), 1)
    low = lane < HEAD_DIM
    first_half = (lane % HEAD_DIM) < (HEAD_DIM // 2)

    def rope(x, cos, sin):
        rot = jnp.where(first_half, pltpu.roll(x, LANES - HEAD_DIM // 2, 1), pltpu.roll(x, HEAD_DIM // 2, 1))
        return x * cos + rot * sin

    def seg_mean(y):
        tot = jnp.sum(y, axis=-1, keepdims=True)
        lo = jnp.sum(jnp.where(low, y, 0.0), axis=-1, keepdims=True)
        return jnp.where(low, lo, tot - lo) * (1.0 / HEAD_DIM)

    for c in range(RET_RB // CHUNK):
        rows = slice(c * CHUNK, (c + 1) * CHUNK)
        cos = cos_ref[rows, :]
        sin = sin_ref[rows, :]
        for p in range(RET_HEADS // 2):
            sl = slice(p * LANES, (p + 1) * LANES)
            qf = rope(q_ref[rows, sl].astype(F32), cos, sin)
            kf = rope(k_ref[rows, sl].astype(F32), cos, sin) * (HEAD_DIM ** -0.5)
            v = v_ref[rows, sl]
            qb = qf.astype(BF16)
            kb = kf.astype(BF16)
            parts = []
            for par in range(2):
                qm = jnp.where(low if par == 0 else jnp.logical_not(low), qb, jnp.zeros_like(qb))
                s = lax.dot_general(qm, kb, _NT, preferred_element_type=F32) * intra_ref[2 * p + par]
                parts.append(jnp.dot(s.astype(BF16), v, preferred_element_type=F32))
            y = jnp.where(low, parts[0], parts[1])
            st = state_ref[p]
            y = y + jnp.dot((qf * qdec_ref[p]).astype(BF16), st.astype(BF16), preferred_element_type=F32)
            kd_t = jnp.transpose(kf * kdec_ref[p]).astype(BF16)
            upd = jnp.dot(kd_t, v, preferred_element_type=F32)
            cm = cmat_ref[p]
            state_ref[p] = cm * st + jnp.where(cm > 0.0, upd, 0.0)
            d = y - seg_mean(y)
            yn = d * lax.rsqrt(seg_mean(d * d) + EPS)
            g = g_ref[rows, sl].astype(F32)
            o_ref[rows, sl] = yn * (g * jax.nn.sigmoid(g))


def _ret_tables(seq):
    half = HEAD_DIM // 2
    inv = ROPE_BASE ** (-jnp.arange(half, dtype=F32) / half)
    ang = jnp.arange(seq).astype(F32)[:, None] * inv[None, :]
    lane = np.arange(LANES)
    sign = np.where((lane % HEAD_DIM) < half, -1.0, 1.0).astype(np.float32)
    cos_t = jnp.cos(ang)[:, lane % half]
    sin_t = jnp.sin(ang)[:, lane % half] * sign[None, :]
    log_g = jnp.log1p(-jnp.power(2.0, -5.0 - jnp.arange(RET_HEADS, dtype=F32)))
    idx = jnp.arange(CHUNK, dtype=F32)
    intra = jnp.exp(log_g[:, None, None] * jnp.abs(idx[:, None] - idx[None, :]))
    kdec = jnp.exp(log_g[:, None] * (CHUNK - 1 - idx)[None, :])
    qdec = jnp.exp(log_g[:, None] * (idx + 1.0)[None, :])
    cdec = jnp.exp(log_g * CHUNK)
    head_of_lane = lane // HEAD_DIM
    npair = RET_HEADS // 2
    kdec_p = jnp.stack([kdec[2 * p + head_of_lane, :].T for p in range(npair)])
    qdec_p = jnp.stack([qdec[2 * p + head_of_lane, :].T for p in range(npair)])
    same = jnp.asarray(head_of_lane[:, None] == head_of_lane[None, :])
    cmat = jnp.stack([jnp.where(same, cdec[2 * p + head_of_lane][:, None], 0.0) for p in range(npair)])
    return cos_t, sin_t, intra, kdec_p, qdec_p, cmat


def _retention(qkvg, tables, batch, seq):
    t = qkvg.shape[0]
    nb = seq // RET_RB
    w = RET_WIDTH
    cos_t, sin_t, intra, kdec_p, qdec_p, cmat = tables
    npair = RET_HEADS // 2

    def col(cb):
        return pl.BlockSpec((RET_RB, w), lambda b, i: (b * nb + i, cb))

    const3 = lambda b, i: (0, 0, 0)
    return pl.pallas_call(
        _ret_kernel,
        out_shape=jax.ShapeDtypeStruct((t, w), F32),
        grid=(batch, nb),
        in_specs=[
            col(0), col(1), col(2), col(3),
            pl.BlockSpec((RET_RB, LANES), lambda b, i: (i, 0)),
            pl.BlockSpec((RET_RB, LANES), lambda b, i: (i, 0)),
            pl.BlockSpec((RET_HEADS, CHUNK, CHUNK), const3),
            pl.BlockSpec((npair, CHUNK, LANES), const3),
            pl.BlockSpec((npair, CHUNK, LANES), const3),
            pl.BlockSpec((npair, LANES, LANES), const3),
        ],
        out_specs=pl.BlockSpec((RET_RB, w), lambda b, i: (b * nb + i, 0)),
        scratch_shapes=[pltpu.VMEM((npair, LANES, LANES), F32)],
        compiler_params=_cparams(("arbitrary", "arbitrary")),
        name="retention",
    )(qkvg, qkvg, qkvg, qkvg, cos_t, sin_t, intra, kdec_p, qdec_p, cmat)


def _ssm_matrices(lam_re, lam_im, log_dt, b_re, b_im, c_re, c_im):
    dt = jnp.exp(log_dt)[:, None]
    lr, li = lam_re, lam_im
    mag = jnp.exp(lr * dt)
    ab_re, ab_im = mag * jnp.cos(li * dt), mag * jnp.sin(li * dt)
    den = lr * lr + li * li
    nr, ni = ab_re - 1.0, ab_im
    f_re = (nr * lr + ni * li) / den
    f_im = (ni * lr - nr * li) / den
    bb_re = f_re[..., None] * b_re - f_im[..., None] * b_im
    bb_im = f_re[..., None] * b_im + f_im[..., None] * b_re
    tau = jnp.arange(SSM_LC + 1, dtype=F32)[:, None, None]
    pmag = jnp.exp(tau * (lr * dt)[None])
    pw_re = pmag * jnp.cos(tau * (li * dt)[None])
    pw_im = pmag * jnp.sin(tau * (li * dt)[None])
    hi = lax.Precision.HIGHEST
    cb_re = c_re[None] * pw_re[:SSM_LC, :, None, :] - c_im[None] * pw_im[:SSM_LC, :, None, :]
    cb_im = c_re[None] * pw_im[:SSM_LC, :, None, :] + c_im[None] * pw_re[:SSM_LC, :, None, :]
    taps = (jnp.einsum('tgop,gpc->tgoc', cb_re, bb_re, precision=hi)
            - jnp.einsum('tgop,gpc->tgoc', cb_im, bb_im, precision=hi))
    s_idx = np.arange(SSM_LC)[:, None]
    t_idx = np.arange(SSM_LC)[None, :]
    lag = t_idx - s_idx
    tt = jnp.where(jnp.asarray(lag >= 0)[:, :, None, None, None], taps[np.clip(lag, 0, SSM_LC - 1)], 0.0)
    eye_g = jnp.eye(SSM_GROUPS, dtype=F32)
    toep = jnp.einsum('stgoc,gh->sgctho', tt, eye_g).reshape(SSM_FLAT, SSM_FLAT)
    rev = pw_re[SSM_LC - 1 - np.arange(SSM_LC)], pw_im[SSM_LC - 1 - np.arange(SSM_LC)]
    be_re = rev[0][..., None] * bb_re[None] - rev[1][..., None] * bb_im[None]
    be_im = rev[0][..., None] * bb_im[None] + rev[1][..., None] * bb_re[None]
    bend_re = jnp.einsum('sgpc,gh->sgchp', be_re, eye_g).reshape(SSM_FLAT, SSM_NSTATE)
    bend_im = jnp.einsum('sgpc,gh->sgchp', be_im, eye_g).reshape(SSM_FLAT, SSM_NSTATE)
    cp_re = c_re[None] * pw_re[1:, :, None, :] - c_im[None] * pw_im[1:, :, None, :]
    cp_im = c_re[None] * pw_im[1:, :, None, :] + c_im[None] * pw_re[1:, :, None, :]
    cpow_re = jnp.einsum('tgop,gh->hptgo', cp_re, eye_g).reshape(SSM_NSTATE, SSM_FLAT)
    cpow_im = jnp.einsum('tgop,gh->hptgo', cp_im, eye_g).reshape(SSM_NSTATE, SSM_FLAT)
    w_first = jnp.concatenate([toep, bend_re, bend_im], axis=1).astype(BF16)
    w_cross = jnp.concatenate([cpow_re, -cpow_im], axis=0).astype(BF16)
    a_chunk = jnp.stack([pw_re[SSM_LC].reshape(-1), pw_im[SSM_LC].reshape(-1)])
    return w_first, w_cross, a_chunk


def _mm_kernel(a_ref, b_ref, o_ref):
    o_ref[...] = jnp.dot(a_ref[...].astype(BF16), b_ref[...], preferred_element_type=F32)


def _mm_add_kernel(a_ref, b_ref, c_ref, o_ref):
    o_ref[...] = c_ref[...] + jnp.dot(a_ref[...].astype(BF16), b_ref[...], preferred_element_type=F32)


def _matmul(a, b, tm, tn, add=None, add_colblk=0, name="matmul"):
    m, k = a.shape
    n = b.shape[1]
    in_specs = [pl.BlockSpec((tm, k), lambda j, i: (i, 0)), pl.BlockSpec((k, tn), lambda j, i: (0, j))]
    args = [a, b]
    kern = _mm_kernel
    if add is not None:
        in_specs.append(pl.BlockSpec((tm, tn), lambda j, i: (i, j + add_colblk)))
        args.append(add)
        kern = _mm_add_kernel
    return pl.pallas_call(
        kern,
        out_shape=jax.ShapeDtypeStruct((m, n), F32),
        grid=(n // tn, m // tm),
        in_specs=in_specs,
        out_specs=pl.BlockSpec((tm, tn), lambda j, i: (i, j)),
        compiler_params=_cparams(("arbitrary", "arbitrary")),
        name=name,
    )(*args)


def _scan_kernel(e_ref, a_ref, o_ref):
    ar = a_ref[0:1, :]
    ai = a_ref[1:2, :]
    n = e_ref.shape[0]
    ns = SSM_NSTATE

    def body(r, carry):
        xr, xi = carry
        o_ref[pl.ds(r, 1), 0:ns] = xr
        o_ref[pl.ds(r, 1), ns:2 * ns] = xi
        er = e_ref[pl.ds(r, 1), 0:ns]
        ei = e_ref[pl.ds(r, 1), ns:2 * ns]
        return ar * xr - ai * xi + er, ar * xi + ai * xr + ei

    zero = jnp.zeros((1, ns), F32)
    lax.fori_loop(0, n, body, (zero, zero), unroll=8)


def _ssm_scan(first_out, a_chunk, batch, rows_per_batch):
    n = first_out.shape[0]
    return pl.pallas_call(
        _scan_kernel,
        out_shape=jax.ShapeDtypeStruct((n, 2 * SSM_NSTATE), F32),
        grid=(batch,),
        in_specs=[pl.BlockSpec((rows_per_batch, 2 * SSM_NSTATE), lambda b: (b, 1)),
                  pl.BlockSpec((2, SSM_NSTATE), lambda b: (0, 0))],
        out_specs=pl.BlockSpec((rows_per_batch, 2 * SSM_NSTATE), lambda b: (b, 0)),
        compiler_params=_cparams(("arbitrary",)),
        name="ssm_scan",
    )(first_out, a_chunk)


def _ssm(u, mats, batch, seq):
    w_first, w_cross, a_chunk = mats
    t = u.shape[0]
    n = t // SSM_LC
    u_flat = u.reshape(n, SSM_FLAT)
    tm = min(512, n)
    first = _matmul(u_flat, w_first, tm, 1024, name="ssm_intra")
    x_prev = _ssm_scan(first, a_chunk, batch, seq // SSM_LC)
    y = _matmul(x_prev, w_cross, tm, 1024, add=first, name="ssm_cross")
    return y.reshape(t, SSM_WIDTH)


def _merge_kernel(x_ref, att_ref, ret_ref, y_ref, u_ref, mod_ref, gain_ref, d_ref, wglu_ref,
                  wo_a_ref, wo_r_ref, wo_s_ref, o_ref):
    ya = y_ref[...] + d_ref[...] * u_ref[...].astype(F32)
    z = jnp.dot(_gelu(ya).astype(BF16), wglu_ref[...], preferred_element_type=F32)
    s = z[:, :SSM_WIDTH] * jax.nn.sigmoid(z[:, SSM_WIDTH:])
    gain = gain_ref[...]
    ma = (_rms_rows(att_ref[...]) * gain[:, :ATT_WIDTH]).astype(BF16)
    mr = (ret_ref[...] * gain[:, ATT_WIDTH:ATT_WIDTH + RET_WIDTH]).astype(BF16)
    ms = (_rms_rows(s) * gain[:, ATT_WIDTH + RET_WIDTH:]).astype(BF16)
    acc = jnp.dot(ma, wo_a_ref[...], preferred_element_type=F32)
    acc = acc + jnp.dot(mr, wo_r_ref[...], preferred_element_type=F32)
    acc = acc + jnp.dot(ms, wo_s_ref[...], preferred_element_type=F32)
    o_ref[...] = x_ref[...] + mod_ref[0, 2:3, :] * acc


def _merge(x2, att, ret, y_ssm, u, mod_l, gain, d_skip, w_glu, w_out, seq):
    t = x2.shape[0]
    tm = TM_PROJ
    per_b = seq // tm
    row = lambda i: (i, 0)
    const = lambda i: (0, 0)
    wo_a, wo_r, wo_s = w_out[:ATT_WIDTH], w_out[ATT_WIDTH:ATT_WIDTH + RET_WIDTH], w_out[ATT_WIDTH + RET_WIDTH:]
    return pl.pallas_call(
        _merge_kernel,
        out_shape=jax.ShapeDtypeStruct((t, D_MODEL), F32),
        grid=(t // tm,),
        in_specs=[
            pl.BlockSpec((tm, D_MODEL), row),
            pl.BlockSpec((tm, ATT_WIDTH), row),
            pl.BlockSpec((tm, RET_WIDTH), row),
            pl.BlockSpec((tm, SSM_WIDTH), row),
            pl.BlockSpec((tm, SSM_WIDTH), row),
            pl.BlockSpec((1, 6, D_MODEL), lambda i: (i // per_b, 0, 0)),
            pl.BlockSpec((1, D_MODEL), const),
            pl.BlockSpec((1, SSM_WIDTH), const),
            pl.BlockSpec((SSM_WIDTH, 2 * SSM_WIDTH), const),
            pl.BlockSpec((ATT_WIDTH, D_MODEL), const),
            pl.BlockSpec((RET_WIDTH, D_MODEL), const),
            pl.BlockSpec((SSM_WIDTH, D_MODEL), const),
        ],
        out_specs=pl.BlockSpec((tm, D_MODEL), row),
        compiler_params=_cparams(("arbitrary",)),
        name="merge_outproj",
    )(x2, att, ret, y_ssm, u, mod_l, gain, d_skip, w_glu, wo_a, wo_r, wo_s)


def _top_rows(s, k):
    row = lax.broadcasted_iota(jnp.int32, (k, LANES), 0)
    top = jnp.full((k, LANES), -jnp.inf, F32)
    for r in range(k):
        m = jnp.max(s, axis=0, keepdims=True)
        top = jnp.where(row == r, m, top)
        s = jnp.where(s == m, -jnp.inf, s)
    return top


def _router_kernel(x_ref, mod_ref, gain_ref, wq_ref, sk_ref, h_ref, s1_ref, e1_ref, s2_ref, e2_ref, tau_ref,
                   q_scr):
    h = _rms_rows(x_ref[...]) * gain_ref[...]
    h = h * (1.0 + mod_ref[0, 4:5, :]) + mod_ref[0, 3:4, :]
    hb = h.astype(BF16)
    h_ref[...] = hb
    q_scr[...] = lax.dot_general(wq_ref[...], hb, _NT, preferred_element_type=F32).astype(BF16)
    half = PEER_QDIM // 2
    ngroups = TM_ROUTER // LANES
    k = PEER_TOPK
    row8 = lax.broadcasted_iota(jnp.int32, (8, LANES), 0)

    def head_body(hd, carry):
        base = pl.multiple_of(hd * PEER_QDIM, PEER_QDIM)
        sc1 = jnp.dot(sk_ref[hd, 0], q_scr[pl.ds(base, half), :], preferred_element_type=F32)
        sc2 = jnp.dot(sk_ref[hd, 1], q_scr[pl.ds(base + half, half), :], preferred_element_type=F32)
        for g in range(ngroups):
            ls = slice(g * LANES, (g + 1) * LANES)
            a1 = sc1[:, ls]
            a2 = sc2[:, ls]
            t1 = _top_rows(a1, k)
            t2 = _top_rows(a2, k)
            cands = [t1[0:1] + t2]
            for r1 in range(1, 8):
                n2 = k // (r1 + 1)
                c = t1[r1:r1 + 1] + t2[0:8]
                cands.append(jnp.where(row8 < n2, c, -jnp.inf))
            cands.append(t1[8:16] + t2[0:1])
            cand = jnp.concatenate(cands, axis=0)
            work = cand
            tau = jnp.zeros((1, LANES), F32)
            for _ in range(k):
                tau = jnp.max(work, axis=0, keepdims=True)
                work = jnp.where(work == tau, -jnp.inf, work)
            m1 = t1[0:1]
            m2 = t2[0:1]
            z = jnp.sum(jnp.where(cand >= tau, jnp.exp(cand - (m1 + m2)), 0.0), axis=0, keepdims=True)
            s1_ref[g, hd] = a1
            s2_ref[g, hd] = a2
            e1_ref[g, hd] = jnp.exp(a1 - m1) / z
            e2_ref[g, hd] = jnp.exp(a2 - m2)
            tau_ref[g, pl.ds(hd, 1), :] = tau
        return carry

    lax.fori_loop(0, PEER_HEADS, head_body, 0)


def _router(x2, mod_l, gain, wq_t, subkeys, seq):
    t = x2.shape[0]
    tm = TM_ROUTER
    per_b = seq // tm
    ng = tm // LANES
    grp = jax.ShapeDtypeStruct((t // LANES, PEER_HEADS, PEER_NKEYS, LANES), F32)
    grp_spec = pl.BlockSpec((ng, PEER_HEADS, PEER_NKEYS, LANES), lambda i: (i, 0, 0, 0))
    return pl.pallas_call(
        _router_kernel,
        out_shape=(jax.ShapeDtypeStruct((t, D_MODEL), BF16), grp, grp, grp, grp,
                   jax.ShapeDtypeStruct((t // LANES, PEER_HEADS, LANES), F32)),
        grid=(t // tm,),
        in_specs=[
            pl.BlockSpec((tm, D_MODEL), lambda i: (i, 0)),
            pl.BlockSpec((1, 6, D_MODEL), lambda i: (i // per_b, 0, 0)),
            pl.BlockSpec((1, D_MODEL), lambda i: (0, 0)),
            pl.BlockSpec((PEER_HEADS * PEER_QDIM, D_MODEL), lambda i: (0, 0)),
            pl.BlockSpec((PEER_HEADS, 2, PEER_NKEYS, PEER_QDIM // 2), lambda i: (0, 0, 0, 0)),
        ],
        out_specs=(pl.BlockSpec((tm, D_MODEL), lambda i: (i, 0)), grp_spec, grp_spec, grp_spec, grp_spec,
                   pl.BlockSpec((ng, PEER_HEADS, LANES), lambda i: (i, 0, 0))),
        scratch_shapes=[pltpu.VMEM((PEER_HEADS * PEER_QDIM, tm), BF16)],
        compiler_params=_cparams(("arbitrary",)),
        name="peer_router",
    )(x2, mod_l, gain, wq_t, subkeys)


def _peer_kernel(x_ref, h_ref, mod_ref, u_ref, vt_ref, s1_ref, e1_ref, s2_ref, e2_ref, tau_ref, o_ref,
                 at_ref, gt_ref, acc_ref):
    e = pl.program_id(1)
    iblocks = TE_PEER // PEER_NKEYS

    @pl.when(e == 0)
    def _():
        acc_ref[...] = jnp.zeros_like(acc_ref)

    at_ref[...] = lax.dot_general(u_ref[...], h_ref[...], _NT, preferred_element_type=F32)

    for g in range(TM_PEER // LANES):
        ls = slice(g * LANES, (g + 1) * LANES)

        def iblock(il, carry, g=g, ls=ls):
            i = e * iblocks + il
            r0 = pl.multiple_of(il * PEER_NKEYS, PEER_NKEYS)
            a = at_ref[pl.ds(r0, PEER_NKEYS), ls]
            w = jnp.zeros((PEER_NKEYS, LANES), F32)
            for hd in range(PEER_HEADS):
                s1 = s1_ref[g, hd, pl.ds(i, 1), :]
                e1 = e1_ref[g, hd, pl.ds(i, 1), :]
                keep = (s1 + s2_ref[g, hd]) >= tau_ref[g, hd:hd + 1, :]
                w = w + jnp.where(keep, e1 * e2_ref[g, hd], 0.0)
            gt_ref[pl.ds(r0, PEER_NKEYS), ls] = (_gelu(a) * w).astype(BF16)
            return carry

        lax.fori_loop(0, iblocks, iblock, 0)

    acc_ref[...] += jnp.dot(vt_ref[...], gt_ref[...], preferred_element_type=F32)

    @pl.when(e == pl.num_programs(1) - 1)
    def _():
        o_ref[...] = x_ref[...] + mod_ref[0, 5:6, :] * jnp.transpose(acc_ref[...])


def _peer(x2, hb, mod_l, u_b, vt_b, s1, e1, s2, e2, tau, seq):
    t = x2.shape[0]
    tm, te = TM_PEER, TE_PEER
    per_b = seq // tm
    ng = tm // LANES
    grp_spec = pl.BlockSpec((ng, PEER_HEADS, PEER_NKEYS, LANES), lambda i, e: (i, 0, 0, 0))
    return pl.pallas_call(
        _peer_kernel,
        out_shape=jax.ShapeDtypeStruct((t, D_MODEL), F32),
        grid=(t // tm, PEER_N // te),
        in_specs=[
            pl.BlockSpec((tm, D_MODEL), lambda i, e: (i, 0)),
            pl.BlockSpec((tm, D_MODEL), lambda i, e: (i, 0)),
            pl.BlockSpec((1, 6, D_MODEL), lambda i, e: (i // per_b, 0, 0)),
            pl.BlockSpec((te, D_MODEL), lambda i, e: (e, 0)),
            pl.BlockSpec((D_MODEL, te), lambda i, e: (0, e)),
            grp_spec, grp_spec, grp_spec, grp_spec,
            pl.BlockSpec((ng, PEER_HEADS, LANES), lambda i, e: (i, 0, 0)),
        ],
        out_specs=pl.BlockSpec((tm, D_MODEL), lambda i, e: (i, 0)),
        scratch_shapes=[pltpu.VMEM((te, tm), F32), pltpu.VMEM((te, tm), BF16), pltpu.VMEM((D_MODEL, tm), F32)],
        compiler_params=_cparams(("arbitrary", "arbitrary")),
        name="peer_experts",
    )(x2, hb, mod_l, u_b, vt_b, s1, e1, s2, e2, tau)


def _final_kernel(x_ref, gain_ref, o_ref):
    o_ref[...] = _rms_rows(x_ref[...]) * gain_ref[...]


def _final_norm(x2, gain):
    t = x2.shape[0]
    tm = TM_PROJ
    return pl.pallas_call(
        _final_kernel,
        out_shape=jax.ShapeDtypeStruct((t, D_MODEL), F32),
        grid=(t // tm,),
        in_specs=[pl.BlockSpec((tm, D_MODEL), lambda i: (i, 0)), pl.BlockSpec((1, D_MODEL), lambda i: (0, 0))],
        out_specs=pl.BlockSpec((tm, D_MODEL), lambda i: (i, 0)),
        compiler_params=_cparams(("arbitrary",)),
        name="final_norm",
    )(x2, gain)


def kernel(x, c, w_ada, b_ada, norm_mix, norm_ffn, w_in, rel_bias, ssm_lambda_re, ssm_lambda_im, ssm_log_dt, ssm_b_re, ssm_b_im, ssm_c_re, ssm_c_im, ssm_d, w_glu, merge_gain, w_out, peer_w_q, peer_subkeys, peer_u, peer_v, final_norm):
    batch, seq, dm = x.shape
    assert dm == D_MODEL and seq % TM_PROJ == 0 and seq % RET_RB == 0 and seq % TM_PEER == 0
    t = batch * seq
    x2 = x.reshape(t, dm)
    mod = _ada_mod(c, w_ada, b_ada)
    ret_tables = _ret_tables(seq)
    a_end = 3 * ATT_WIDTH
    r_end = a_end + 4 * RET_WIDTH
    for layer in range(DEPTH):
        mod_l = mod[layer]
        w_l = w_in[layer].astype(BF16)
        qkv_a, qkvg_r, u = _inproj(x2, mod_l, norm_mix[layer][None], w_l[:, :a_end], w_l[:, a_end:r_end],
                                   w_l[:, r_end:], seq)
        att = _attention(qkv_a, _att_bias(rel_bias[layer]), batch, seq)
        ret = _retention(qkvg_r, ret_tables, batch, seq)
        mats = _ssm_matrices(ssm_lambda_re[layer], ssm_lambda_im[layer], ssm_log_dt[layer], ssm_b_re[layer],
                             ssm_b_im[layer], ssm_c_re[layer], ssm_c_im[layer])
        y_ssm = _ssm(u, mats, batch, seq)
        x2 = _merge(x2, att, ret, y_ssm, u, mod_l, merge_gain[layer][None], ssm_d[layer][None],
                    w_glu[layer].astype(BF16), w_out[layer].astype(BF16), seq)
        hb, s1, e1, s2, e2, tau = _router(x2, mod_l, norm_ffn[layer][None], peer_w_q[layer].T.astype(BF16),
                                          peer_subkeys[layer].astype(BF16), seq)
        x2 = _peer(x2, hb, mod_l, peer_u[layer].astype(BF16), peer_v[layer].T.astype(BF16),
                   s1, e1, s2, e2, tau, seq)
    return _final_norm(x2, final_norm[None]).reshape(batch, seq, dm)
```
